```python
import math
import jax, jax.numpy as jnp
from jax import lax
import numpy as np

D_MODEL = 1024
BATCH = 16
SEQ = 2048
DEPTH = 1

ATTN_HEAD_DIM = 128
ATTN_HEADS = D_MODEL // ATTN_HEAD_DIM
D_ATTN = ATTN_HEADS * ATTN_HEAD_DIM
MOBA_BLOCK = 256
MOBA_TOPK = 3
MOBA_QCHUNK = 8
SSM_HEAD_DIM = 64
D_SSM = D_MODEL
SSM_HEADS = D_SSM // SSM_HEAD_DIM
SSM_GROUPS = 2
SSM_STATE = 128
SSM_CONV = 4
SSM_CHUNK = 256
D_XBC = D_SSM + 2 * SSM_GROUPS * SSM_STATE
D_MIX = D_ATTN + D_SSM
D_IN = 3 * D_ATTN + D_SSM + D_XBC + SSM_HEADS
D_FF = ((8 * D_MODEL // 3) + 255) // 256 * 256
FFN_CONV = 3
N_MOD = 6
EPS = 1e-6
DT_MIN = 1e-3
DT_MAX = 1e-1

kernel_name = "hybrid_ssd_moba_convffn_adaln"


def rms_norm(x, g):
    xf = x.astype(jnp.float32)
    y = xf * lax.rsqrt(jnp.mean(xf * xf, axis=-1, keepdims=True) + EPS)
    return (y * g.astype(jnp.float32)).astype(x.dtype)


def causal_dwconv(x, w, b):
    width, ch = w.shape
    y = lax.conv_general_dilated(
        x, w[:, None, :], window_strides=(1,), padding=[(width - 1, 0)],
        dimension_numbers=("NWC", "WIO", "NWC"), feature_group_count=ch)
    return y + b


def pad_seq(a, axis, mult):
    extra = (-a.shape[axis]) % mult
    if extra == 0:
        return a
    widths = [(0, 0)] * a.ndim
    widths[axis] = (0, extra)
    return jnp.pad(a, widths)


def ssd_chunked(xdt, da, bm, cm):
    bsz, seq, nh, hp = xdt.shape
    ng, ns = bm.shape[2], bm.shape[3]
    hpg = nh // ng
    xdt, da, bm, cm = (pad_seq(t.astype(jnp.float32), 1, SSM_CHUNK) for t in (xdt, da, bm, cm))
    nc = xdt.shape[1] // SSM_CHUNK
    xc = xdt.reshape(bsz, nc, SSM_CHUNK, ng, hpg, hp)
    bc = bm.reshape(bsz, nc, SSM_CHUNK, ng, ns)
    cc = cm.reshape(bsz, nc, SSM_CHUNK, ng, ns)
    a = da.reshape(bsz, nc, SSM_CHUNK, ng, hpg).transpose(0, 3, 4, 1, 2)
    a_cum = jnp.cumsum(a, axis=-1)
    causal = jnp.tril(jnp.ones((SSM_CHUNK, SSM_CHUNK), dtype=bool))
    decay_in = jnp.exp(jnp.where(causal, a_cum[..., :, None] - a_cum[..., None, :], -jnp.inf))
    cb = jnp.einsum("bclgn,bcsgn->bgcls", cc, bc)
    y_diag = jnp.einsum("bgcls,bgecls,bcsgep->bclgep", cb, decay_in, xc)
    decay_to_end = jnp.exp(a_cum[..., -1:] - a_cum)
    states = jnp.einsum("bclgn,bgecl,bclgep->bcgepn", bc, decay_to_end, xc)
    chunk_decay = jnp.exp(a_cum[..., -1])

    def carry_state(h_prev, inp):
        st, dec = inp
        return h_prev * dec[..., None, None] + st, h_prev

    h0 = jnp.zeros((bsz, ng, hpg, hp, ns), jnp.float32)
    _, h_in = lax.scan(carry_state, h0, (jnp.moveaxis(states, 1, 0), jnp.moveaxis(chunk_decay, 3, 0)))
    h_in = jnp.moveaxis(h_in, 0, 1)
    y_off = jnp.einsum("bclgn,bcgepn,bgecl->bclgep", cc, h_in, jnp.exp(a_cum))
    return (y_diag + y_off).reshape(bsz, nc * SSM_CHUNK, nh, hp)[:, :seq]


def moba_attention(q, k, v):
    bsz, nh, seq, dh = q.shape
    q, k, v = (pad_seq(t, 2, MOBA_BLOCK) for t in (q, k, v))
    sp = q.shape[2]
    nb = sp // MOBA_BLOCK
    n_sel = min(MOBA_TOPK, nb)
    scale = dh ** -0.5
    kb = k.reshape(bsz, nh, nb, MOBA_BLOCK, dh)
    vb = v.reshape(bsz, nh, nb, MOBA_BLOCK, dh)
    k_mean = jnp.mean(kb.astype(jnp.float32), axis=3)
    q_blk = jnp.arange(sp) // MOBA_BLOCK
    gate = jnp.einsum("bhsd,bhnd->bhsn", q.astype(jnp.float32), k_mean)
    fully_past = jnp.arange(nb)[None, :] < q_blk[:, None]
    gate = jnp.where(fully_past, gate, -jnp.inf)
    _, sel = lax.top_k(gate, n_sel)
    n_steps = sp // MOBA_QCHUNK
    q_steps = q.reshape(bsz, nh, n_steps, MOBA_QCHUNK, dh).transpose(2, 0, 1, 3, 4)
    sel_steps = sel.reshape(bsz, nh, n_steps, MOBA_QCHUNK, n_sel).transpose(2, 0, 1, 3, 4)
    gather_blocks = jax.vmap(jax.vmap(lambda blocks, idx: blocks[idx]))

    def step(args):
        qc, sc, i = args
        t0 = i * MOBA_QCHUNK
        own = t0 // MOBA_BLOCK
        pos = t0 + jnp.arange(MOBA_QCHUNK)
        kpos = own * MOBA_BLOCK + jnp.arange(MOBA_BLOCK)
        k_own = lax.dynamic_index_in_dim(kb, own, axis=2, keepdims=False)
        v_own = lax.dynamic_index_in_dim(vb, own, axis=2, keepdims=False)
        s_own = jnp.einsum("bhqd,bhkd->bhqk", qc, k_own).astype(jnp.float32) * scale
        s_own = jnp.where(kpos[None, :] <= pos[:, None], s_own, -jnp.inf)
        k_sel = gather_blocks(kb, sc)
        v_sel = gather_blocks(vb, sc)
        s_sel = jnp.einsum("bhqd,bhqnkd->bhqnk", qc, k_sel).astype(jnp.float32) * scale
        valid = jnp.arange(n_sel) < own
        s_sel = jnp.where(valid[:, None], s_sel, -jnp.inf)
        logits = jnp.concatenate(
            [s_own, s_sel.reshape(bsz, nh, MOBA_QCHUNK, n_sel * MOBA_BLOCK)], axis=-1)
        p = jax.nn.softmax(logits, axis=-1).astype(v.dtype)
        p_own = p[..., :MOBA_BLOCK]
        p_sel = p[..., MOBA_BLOCK:].reshape(bsz, nh, MOBA_QCHUNK, n_sel, MOBA_BLOCK)
        return (jnp.einsum("bhqk,bhkd->bhqd", p_own, v_own)
                + jnp.einsum("bhqnk,bhqnkd->bhqd", p_sel, v_sel))

    out = lax.map(step, (q_steps, sel_steps, jnp.arange(n_steps)))
    return out.transpose(1, 2, 0, 3, 4).reshape(bsz, nh, sp, dh)[:, :, :seq]


def hybrid_token_mixer(h, w_in, q_norm_g, k_norm_g, conv_ssm_w, conv_ssm_b, dt_bias,
                       a_log, d_skip, ssm_norm_g, attn_norm_g, w_out):
    bsz, seq, _ = h.shape
    proj = h @ w_in
    o1 = D_ATTN
    o2 = 2 * D_ATTN
    o3 = 3 * D_ATTN
    o4 = o3 + D_SSM
    o5 = o4 + D_XBC
    q, k, v, z, xbc, dt = jnp.split(proj, [o1, o2, o3, o4, o5], axis=-1)
    q = rms_norm(q.reshape(bsz, seq, ATTN_HEADS, ATTN_HEAD_DIM), q_norm_g).transpose(0, 2, 1, 3)
    k = rms_norm(k.reshape(bsz, seq, ATTN_HEADS, ATTN_HEAD_DIM), k_norm_g).transpose(0, 2, 1, 3)
    v = v.reshape(bsz, seq, ATTN_HEADS, ATTN_HEAD_DIM).transpose(0, 2, 1, 3)
    y_attn = moba_attention(q, k, v).transpose(0, 2, 1, 3).reshape(bsz, seq, D_ATTN)
    y_attn = rms_norm(y_attn, attn_norm_g)
    xbc = jax.nn.silu(causal_dwconv(xbc, conv_ssm_w, conv_ssm_b))
    xs, bm, cm = jnp.split(xbc, [D_SSM, D_SSM + SSM_GROUPS * SSM_STATE], axis=-1)
    xs = xs.reshape(bsz, seq, SSM_HEADS, SSM_HEAD_DIM).astype(jnp.float32)
    dt = jax.nn.softplus((dt + dt_bias).astype(jnp.float32))
    a = -jnp.exp(a_log.astype(jnp.float32))
    y = ssd_chunked(xs * dt[..., None], dt * a,
                    bm.reshape(bsz, seq, SSM_GROUPS, SSM_STATE),
                    cm.reshape(bsz, seq, SSM_GROUPS, SSM_STATE))
    y = y + xs * d_skip.astype(jnp.float32)[:, None]
    y = y.reshape(bsz, seq, D_SSM) * jax.nn.silu(z.astype(jnp.float32))
    y_ssm = rms_norm(y.reshape(bsz, seq, SSM_GROUPS, D_SSM // SSM_GROUPS),
                     ssm_norm_g.reshape(SSM_GROUPS, D_SSM // SSM_GROUPS))
    y_ssm = y_ssm.reshape(bsz, seq, D_SSM).astype(h.dtype)
    return jnp.concatenate([y_attn, y_ssm], axis=-1) @ w_out


def conv_ffn(h, w_up, conv_w, conv_b, w_down):
    u = causal_dwconv(h @ w_up, conv_w, conv_b)
    g, val = jnp.split(u, 2, axis=-1)
    return (jax.nn.silu(g) * val) @ w_down


def setup_inputs(seed: int = 0) -> dict:
    key = jax.random.key(seed)
    ks = jax.random.split(key, 24)
    f32 = jnp.float32

    def nrm(k, shape, s):
        return jax.random.normal(k, shape, f32) * s

    def gain(k, shape):
        return 1.0 + 0.02 * jax.random.normal(k, shape, f32)

    dt0 = jnp.exp(jax.random.uniform(ks[10], (DEPTH, SSM_HEADS), f32)
                  * (math.log(DT_MAX) - math.log(DT_MIN)) + math.log(DT_MIN))
    return {
        "x": jax.random.normal(ks[0], (BATCH, SEQ, D_MODEL), f32),
        "c": jax.random.normal(ks[1], (BATCH, D_MODEL), f32),
        "w_ada": nrm(ks[2], (DEPTH, D_MODEL, N_MOD * D_MODEL), 0.5 * D_MODEL ** -0.5),
        "b_ada": nrm(ks[3], (DEPTH, N_MOD * D_MODEL), 0.02),
        "norm1_g": gain(ks[4], (DEPTH, D_MODEL)),
        "w_in": nrm(ks[5], (DEPTH, D_MODEL, D_IN), D_MODEL ** -0.5),
        "q_norm_g": gain(ks[6], (DEPTH, ATTN_HEAD_DIM)),
        "k_norm_g": gain(ks[7], (DEPTH, ATTN_HEAD_DIM)),
        "conv_ssm_w": nrm(ks[8], (DEPTH, SSM_CONV, D_XBC), SSM_CONV ** -0.5),
        "conv_ssm_b": nrm(ks[9], (DEPTH, D_XBC), 0.02),
        "dt_bias": dt0 + jnp.log(-jnp.expm1(-dt0)),
        "a_log": jnp.log(jax.random.uniform(ks[11], (DEPTH, SSM_HEADS), f32, minval=1.0, maxval=16.0)),
        "d_skip": gain(ks[12], (DEPTH, SSM_HEADS)),
        "ssm_norm_g": gain(ks[13], (DEPTH, D_SSM)),
        "attn_norm_g": gain(ks[14], (DEPTH, D_ATTN)),
        "w_out": nrm(ks[15], (DEPTH, D_MIX, D_MODEL), D_MIX ** -0.5),
        "norm2_g": gain(ks[16], (DEPTH, D_MODEL)),
        "w_up": nrm(ks[17], (DEPTH, D_MODEL, 2 * D_FF), D_MODEL ** -0.5),
        "conv_ffn_w": nrm(ks[18], (DEPTH, FFN_CONV, 2 * D_FF), FFN_CONV ** -0.5),
        "conv_ffn_b": nrm(ks[19], (DEPTH, 2 * D_FF), 0.02),
        "w_down": nrm(ks[20], (DEPTH, D_FF, D_MODEL), D_FF ** -0.5),
    }


def reference(x, c, w_ada, b_ada, norm1_g, w_in, q_norm_g, k_norm_g, conv_ssm_w, conv_ssm_b,
              dt_bias, a_log, d_skip, ssm_norm_g, attn_norm_g, w_out, norm2_g, w_up,
              conv_ffn_w, conv_ffn_b, w_down):
    for l in range(DEPTH):
        mod = jax.nn.silu(c) @ w_ada[l] + b_ada[l]
        shift1, scale1, gate1, shift2, scale2, gate2 = (
            m[:, None, :] for m in jnp.split(mod, N_MOD, axis=-1))
        h = rms_norm(x, norm1_g[l]) * (1 + scale1) + shift1
        x = x + gate1 * hybrid_token_mixer(
            h, w_in[l], q_norm_g[l], k_norm_g[l], conv_ssm_w[l], conv_ssm_b[l], dt_bias[l],
            a_log[l], d_skip[l], ssm_norm_g[l], attn_norm_g[l], w_out[l])
        h = rms_norm(x, norm2_g[l]) * (1 + scale2) + shift2
        x = x + gate2 * conv_ffn(h, w_up[l], conv_ffn_w[l], conv_ffn_b[l], w_down[l])
    return x
```

```python
import functools

import jax
import jax.numpy as jnp
from jax import lax
from jax.experimental import pallas as pl
from jax.experimental.pallas import tpu as pltpu

F32 = jnp.float32
BF16 = jnp.bfloat16

EPS = 1e-6
LANES = 128
SUBLANES = 8
VMEM_LIMIT = 56 * 1024 * 1024

ATTN_HEAD_DIM = 128
MOBA_BLOCK = 256
MOBA_TOPK = 3
SSM_HEAD_DIM = 64
SSM_GROUPS = 2
SSM_STATE = 128
SSM_CONV = 4
SSM_CHUNK = 256
FFN_CONV = 3

NT_DIMS = (((1,), (1,)), ((), ()))


def _dot(a, b):
    return jnp.dot(a, b, preferred_element_type=F32)


def _split3(a):
    hi = a.astype(BF16)
    r1 = a - hi.astype(F32)
    mid = r1.astype(BF16)
    lo = (r1 - mid.astype(F32)).astype(BF16)
    return hi, mid, lo


def _dot3_lhs(a, b_bf16):
    hi, mid, lo = _split3(a)
    return _dot(hi, b_bf16) + _dot(mid, b_bf16) + _dot(lo, b_bf16)


def _dot3_rhs(a_bf16, b):
    hi, mid, lo = _split3(b)
    return _dot(a_bf16, hi) + _dot(a_bf16, mid) + _dot(a_bf16, lo)


def _silu(x):
    return x * (1.0 / (1.0 + jnp.exp(-x)))


def _shift_rows(u, prev, k):
    r = pltpu.roll(u, k, axis=0)
    rp = pltpu.roll(prev, k, axis=0)
    row = lax.broadcasted_iota(jnp.int32, prev.shape, 0)
    top = jnp.where(row < k, rp, r[0:SUBLANES])
    return jnp.concatenate([top, r[SUBLANES:]], axis=0)


def _mod_kernel(c_ref, w_ref, b_ref, o_ref):
    sc = _silu(c_ref[...])
    o_ref[...] = jnp.dot(sc, w_ref[...], precision=lax.Precision.HIGHEST,
                         preferred_element_type=F32) + b_ref[...]


def _adaln_mod(c, w_ada, b_ada):
    bsz, d = c.shape
    n = w_ada.shape[1]
    tn = 1536
    return pl.pallas_call(
        _mod_kernel,
        out_shape=jax.ShapeDtypeStruct((bsz, n), F32),
        grid=(n // tn,),
        in_specs=[pl.BlockSpec((bsz, d), lambda j: (0, 0)),
                  pl.BlockSpec((d, tn), lambda j: (0, j)),
                  pl.BlockSpec((1, tn), lambda j: (0, j))],
        out_specs=pl.BlockSpec((bsz, tn), lambda j: (0, j)),
        name="adaln_mod",
    )(c, w_ada, b_ada.reshape(1, n))


def _inproj_kernel(x_ref, g1_ref, sc_ref, sh_ref, w_ref, gq_ref, gk_ref,
                   q_ref, k_ref, v_ref, z_ref, xbc_ref, dt_ref, *, d_attn, d_ssm, d_xbc):
    x = x_ref[...]
    ms = jnp.mean(x * x, axis=-1, keepdims=True)
    h = x * lax.rsqrt(ms + EPS) * g1_ref[...]
    h = h * (1.0 + sc_ref[...]) + sh_ref[...]
    hb = h.astype(BF16)
    n_heads = d_attn // ATTN_HEAD_DIM

    def head_norm(col0, g_ref, out_ref):
        g = g_ref[...]
        for hd in range(n_heads):
            lo = hd * ATTN_HEAD_DIM
            p = _dot(hb, w_ref[:, col0 + lo:col0 + lo + ATTN_HEAD_DIM])
            r = lax.rsqrt(jnp.mean(p * p, axis=-1, keepdims=True) + EPS)
            out_ref[:, lo:lo + ATTN_HEAD_DIM] = (p * r * g).astype(BF16)

    head_norm(0, gq_ref, q_ref)
    head_norm(d_attn, gk_ref, k_ref)
    o2, o3 = 2 * d_attn, 3 * d_attn
    o4 = o3 + d_ssm
    o5 = o4 + d_xbc
    v_ref[...] = _dot(hb, w_ref[:, o2:o3]).astype(BF16)
    z_ref[...] = _dot(hb, w_ref[:, o3:o4]).astype(BF16)
    xbc_ref[...] = _dot(hb, w_ref[:, o4:o5]).astype(BF16)
    dt_ref[...] = _dot(hb, w_ref[:, o5:o5 + LANES])


def _inproj(x, norm1_g, scale1, shift1, w_in_p, q_norm_g, k_norm_g, *, d_attn, d_ssm, d_xbc, tm):
    bsz, seq, d = x.shape
    n_pad = w_in_p.shape[1]
    row = lambda b, s: (b, s, 0)
    per_b = lambda b, s: (b, 0, 0)
    const = lambda b, s: (0, 0)
    kern = functools.partial(_inproj_kernel, d_attn=d_attn, d_ssm=d_ssm, d_xbc=d_xbc)
    return pl.pallas_call(
        kern,
        out_shape=[jax.ShapeDtypeStruct((bsz, seq, d_attn), BF16),
                   jax.ShapeDtypeStruct((bsz, seq, d_attn), BF16),
                   jax.ShapeDtypeStruct((bsz, seq, d_attn), BF16),
                   jax.ShapeDtypeStruct((bsz, seq, d_ssm), BF16),
                   jax.ShapeDtypeStruct((bsz, seq, d_xbc), BF16),
                   jax.ShapeDtypeStruct((bsz, seq, LANES), F32)],
        grid=(bsz, seq // tm),
        in_specs=[pl.BlockSpec((None, tm, d), row),
                  pl.BlockSpec((1, d), const),
                  pl.BlockSpec((None, 1, d), per_b),
                  pl.BlockSpec((None, 1, d), per_b),
                  pl.BlockSpec((d, n_pad), const, pipeline_mode=pl.Buffered(1)),
                  pl.BlockSpec((1, ATTN_HEAD_DIM), const),
                  pl.BlockSpec((1, ATTN_HEAD_DIM), const)],
        out_specs=[pl.BlockSpec((None, tm, d_attn), row),
                   pl.BlockSpec((None, tm, d_attn), row),
                   pl.BlockSpec((None, tm, d_attn), row),
                   pl.BlockSpec((None, tm, d_ssm), row),
                   pl.BlockSpec((None, tm, d_xbc), row),
                   pl.BlockSpec((None, tm, LANES), row)],
        compiler_params=pltpu.CompilerParams(
            dimension_semantics=("parallel", "parallel"), vmem_limit_bytes=VMEM_LIMIT),
        name="inproj",
    )(x, norm1_g.reshape(1, d), scale1, shift1, w_in_p,
      q_norm_g.reshape(1, ATTN_HEAD_DIM), k_norm_g.reshape(1, ATTN_HEAD_DIM))


def _attn_kernel(q_ref, k_ref, v_ref, o_ref, *, seq):
    blk = MOBA_BLOCK
    nb = seq // blk
    scale = ATTN_HEAD_DIM ** -0.5
    neg_inf = float("-inf")

    k_all = k_ref[...]
    pool_n = lax.broadcasted_iota(jnp.int32, (nb, seq), 0)
    pool_s = lax.broadcasted_iota(jnp.int32, (nb, seq), 1)
    lo = pool_n * blk
    pool = jnp.where((pool_s >= lo) & (pool_s < lo + blk), 1.0 / blk, 0.0).astype(BF16)
    k_mean = _dot(pool, k_all)
    v_t = v_ref[...].T

    key_pos = lax.broadcasted_iota(jnp.int32, (blk, blk), 0)
    qry_pos = lax.broadcasted_iota(jnp.int32, (blk, blk), 1)
    causal = key_pos <= qry_pos
    blk_id = lax.broadcasted_iota(jnp.int32, (nb, blk), 0)

    for i in range(nb):
        qi = q_ref[i * blk:(i + 1) * blk, :]
        sel = None
        if i > MOBA_TOPK:
            gate = lax.dot_general(k_mean, qi.astype(F32), NT_DIMS,
                                   precision=lax.Precision.HIGHEST, preferred_element_type=F32)
            past = blk_id < i
            sel = []
            for j in range(i):
                gj = gate[j:j + 1, :]
                beats = ((gate > gj) | ((gate == gj) & (blk_id < j))) & past
                cnt = jnp.sum(beats.astype(F32), axis=0, keepdims=True)
                sel.append(cnt < float(MOBA_TOPK))
        tiles = []
        for j in range(i + 1):
            kj = k_ref[j * blk:(j + 1) * blk, :]
            s = lax.dot_general(kj, qi, NT_DIMS, preferred_element_type=F32)
            if j == i:
                s = jnp.where(causal, s, neg_inf)
            elif sel is not None:
                s = jnp.where(sel[j], s, neg_inf)
            tiles.append(s)
        m = tiles[0]
        for s in tiles[1:]:
            m = jnp.maximum(m, s)
        m = jnp.max(m, axis=0, keepdims=True)
        denom = jnp.zeros((1, blk), F32)
        acc = jnp.zeros((ATTN_HEAD_DIM, blk), F32)
        for j, s in enumerate(tiles):
            p = jnp.exp((s - m) * scale)
            denom = denom + jnp.sum(p, axis=0, keepdims=True)
            acc = acc + _dot(v_t[:, j * blk:(j + 1) * blk], p.astype(BF16))
        out_t = acc * (1.0 / denom)
        o_ref[i * blk:(i + 1) * blk, :] = out_t.T.astype(BF16)


def _moba_attention(q, k, v):
    bsz, seq, d_attn = q.shape
    n_heads = d_attn // ATTN_HEAD_DIM
    spec = pl.BlockSpec((None, seq, ATTN_HEAD_DIM), lambda b, h: (b, 0, h))
    return pl.pallas_call(
        functools.partial(_attn_kernel, seq=seq),
        out_shape=jax.ShapeDtypeStruct((bsz, seq, d_attn), BF16),
        grid=(bsz, n_heads),
        in_specs=[spec, spec, spec],
        out_specs=spec,
        compiler_params=pltpu.CompilerParams(
            dimension_semantics=("parallel", "parallel"), vmem_limit_bytes=VMEM_LIMIT),
        name="moba_attn",
    )(q, k, v)


def _ssd_kernel(xbc_ref, dt_ref, z_ref, cw_ref, cb_ref, dtb_ref, alog_ref, dskip_ref, gn_ref,
                y_ref, carry_ref, state_ref, *, d_ssm, n_heads):
    cl = SSM_CHUNK
    gs = SSM_GROUPS * SSM_STATE
    hpg = n_heads // SSM_GROUPS
    gw = d_ssm // SSM_GROUPS
    chunk = pl.program_id(1)

    @pl.when(chunk == 0)
    def _():
        carry_ref[...] = jnp.zeros_like(carry_ref)
        state_ref[...] = jnp.zeros_like(state_ref)

    xb = xbc_ref[...].astype(F32)
    prev = carry_ref[...]
    cw = cw_ref[...]
    y = xb * cw[SSM_CONV - 1:SSM_CONV, :] + cb_ref[...]
    for k in range(1, SSM_CONV):
        y = y + _shift_rows(xb, prev, k) * cw[SSM_CONV - 1 - k:SSM_CONV - k, :]
    carry_ref[...] = xb[cl - SUBLANES:cl, :]
    xbc = _silu(y)
    xs = xbc[:, :d_ssm]
    bm = xbc[:, d_ssm:d_ssm + gs]
    cm = xbc[:, d_ssm + gs:d_ssm + 2 * gs]

    dtr = dt_ref[...] + dtb_ref[...]
    dt = jnp.maximum(dtr, 0.0) + jnp.log(1.0 + jnp.exp(-jnp.abs(dtr)))
    da = dt * (-jnp.exp(alog_ref[...]))
    r_io = lax.broadcasted_iota(jnp.int32, (cl, cl), 0)
    c_io = lax.broadcasted_iota(jnp.int32, (cl, cl), 1)
    tril = r_io >= c_io
    a_cum = _dot3_rhs(tril.astype(BF16), da)
    a_cum_t = a_cum.T
    dt_t = dt.T
    a_end = a_cum[cl - 1:cl, :]
    e_cum = jnp.exp(a_cum)
    w_dt = jnp.exp(a_end - a_cum) * dt

    e_h = lax.broadcasted_iota(jnp.int32, (LANES, d_ssm), 0)
    e_c = lax.broadcasted_iota(jnp.int32, (LANES, d_ssm), 1)
    c_lo = e_h * SSM_HEAD_DIM
    expand = ((e_c >= c_lo) & (e_c < c_lo + SSM_HEAD_DIM)).astype(BF16)
    e_cum_x = _dot3_lhs(e_cum, expand)
    w_dt_x = _dot3_lhs(w_dt, expand)

    lane = lax.broadcasted_iota(jnp.int32, (cl, LANES), 1)
    lower_half = lane < SSM_HEAD_DIM
    heads_per_tile = LANES // SSM_HEAD_DIM

    y_parts = []
    for g in range(SSM_GROUPS):
        bg = bm[:, g * SSM_STATE:(g + 1) * SSM_STATE]
        cg = cm[:, g * SSM_STATE:(g + 1) * SSM_STATE].astype(BF16)
        cb = lax.dot_general(cg, bg.astype(BF16), NT_DIMS, preferred_element_type=F32)
        cb = jnp.where(tril, cb, 0.0)
        xs_g = xs[:, g * gw:(g + 1) * gw]
        tiles = []
        for t in range(gw // LANES):
            x_tile = xs_g[:, t * LANES:(t + 1) * LANES]
            acc = None
            for half in range(heads_per_tile):
                hd = g * hpg + t * heads_per_tile + half
                diff = jnp.minimum(a_cum[:, hd:hd + 1] - a_cum_t[hd:hd + 1, :], 0.0)
                mh = cb * jnp.exp(diff) * dt_t[hd:hd + 1, :]
                keep = lower_half if half == 0 else jnp.logical_not(lower_half)
                xh = jnp.where(keep, x_tile, 0.0).astype(BF16)
                part = _dot(mh.astype(BF16), xh)
                acc = part if acc is None else acc + part
            tiles.append(acc)
        y_diag = jnp.concatenate(tiles, axis=-1)
        h_in = state_ref[g]
        e_x = e_cum_x[:, g * gw:(g + 1) * gw]
        y_off = _dot(cg, h_in.astype(BF16)) * e_x
        xw = (xs_g * w_dt_x[:, g * gw:(g + 1) * gw]).astype(BF16)
        s_new = _dot(bg.T.astype(BF16), xw)
        state_ref[g] = h_in * e_x[cl - 1:cl, :] + s_new
        y_parts.append(y_diag + y_off)

    z = z_ref[...].astype(F32)
    for g in range(SSM_GROUPS):
        sl = slice(g * gw, (g + 1) * gw)
        yg = y_parts[g] + xs[:, sl] * dskip_ref[:, sl]
        yg = yg * _silu(z[:, sl])
        r = lax.rsqrt(jnp.mean(yg * yg, axis=-1, keepdims=True) + EPS)
        y_ref[:, sl] = (yg * r * gn_ref[:, sl]).astype(BF16)


def _ssd(xbc, dt, z, conv_w, conv_b, dt_bias_p, a_log_p, d_skip_x, ssm_norm_g, *, n_heads):
    bsz, seq, d_xbc = xbc.shape
    d_ssm = z.shape[-1]
    cl = SSM_CHUNK
    row = lambda b, c: (b, c, 0)
    const = lambda b, c: (0, 0)
    kern = functools.partial(_ssd_kernel, d_ssm=d_ssm, n_heads=n_heads)
    return pl.pallas_call(
        kern,
        out_shape=jax.ShapeDtypeStruct((bsz, seq, d_ssm), BF16),
        grid=(bsz, seq // cl),
        in_specs=[pl.BlockSpec((None, cl, d_xbc), row),
                  pl.BlockSpec((None, cl, LANES), row),
                  pl.BlockSpec((None, cl, d_ssm), row),
                  pl.BlockSpec((SSM_CONV, d_xbc), const),
                  pl.BlockSpec((1, d_xbc), const),
                  pl.BlockSpec((1, LANES), const),
                  pl.BlockSpec((1, LANES), const),
                  pl.BlockSpec((1, d_ssm), const),
                  pl.BlockSpec((1, d_ssm), const)],
        out_specs=pl.BlockSpec((None, cl, d_ssm), row),
        scratch_shapes=[pltpu.VMEM((SUBLANES, d_xbc), F32),
                        pltpu.VMEM((SSM_GROUPS, SSM_STATE, d_ssm // SSM_GROUPS), F32)],
        compiler_params=pltpu.CompilerParams(
            dimension_semantics=("parallel", "arbitrary"), vmem_limit_bytes=VMEM_LIMIT),
        name="ssd",
    )(xbc, dt, z, conv_w, conv_b.reshape(1, d_xbc), dt_bias_p, a_log_p, d_skip_x,
      ssm_norm_g.reshape(1, d_ssm))


def _outproj_kernel(ya_ref, ys_ref, x_ref, ga_ref, wa_ref, ws_ref, gate_ref, g2_ref, sc_ref, sh_ref,
                    x1_ref, h2_ref):
    ya = ya_ref[...].astype(F32)
    r = lax.rsqrt(jnp.mean(ya * ya, axis=-1, keepdims=True) + EPS)
    ya = (ya * r * ga_ref[...]).astype(BF16)
    mix = _dot(ya, wa_ref[...]) + _dot(ys_ref[...], ws_ref[...])
    x1 = x_ref[...] + gate_ref[...] * mix
    x1_ref[...] = x1
    ms = jnp.mean(x1 * x1, axis=-1, keepdims=True)
    h = x1 * lax.rsqrt(ms + EPS) * g2_ref[...]
    h2_ref[...] = (h * (1.0 + sc_ref[...]) + sh_ref[...]).astype(BF16)


def _outproj(y_attn, y_ssm, x, attn_norm_g, w_out_a, w_out_s, gate1, norm2_g, scale2, shift2, *, tm):
    bsz, seq, d = x.shape
    d_attn = y_attn.shape[-1]
    d_ssm = y_ssm.shape[-1]
    row = lambda b, s: (b, s, 0)
    per_b = lambda b, s: (b, 0, 0)
    const = lambda b, s: (0, 0)
    return pl.pallas_call(
        _outproj_kernel,
        out_shape=[jax.ShapeDtypeStruct((bsz, seq, d), F32),
                   jax.ShapeDtypeStruct((bsz, seq, d), BF16)],
        grid=(bsz, seq // tm),
        in_specs=[pl.BlockSpec((None, tm, d_attn), row),
                  pl.BlockSpec((None, tm, d_ssm), row),
                  pl.BlockSpec((None, tm, d), row),
                  pl.BlockSpec((1, d_attn), const),
                  pl.BlockSpec((d_attn, d), const),
                  pl.BlockSpec((d_ssm, d), const),
                  pl.BlockSpec((None, 1, d), per_b),
                  pl.BlockSpec((1, d), const),
                  pl.BlockSpec((None, 1, d), per_b),
                  pl.BlockSpec((None, 1, d), per_b)],
        out_specs=[pl.BlockSpec((None, tm, d), row),
                   pl.BlockSpec((None, tm, d), row)],
        compiler_params=pltpu.CompilerParams(
            dimension_semantics=("parallel", "parallel"), vmem_limit_bytes=VMEM_LIMIT),
        name="outproj",
    )(y_attn, y_ssm, x, attn_norm_g.reshape(1, d_attn), w_out_a, w_out_s, gate1,
      norm2_g.reshape(1, d), scale2, shift2)


def _ffn_chunks(d_ff):
    step = 768
    out, c = [], 0
    while c < d_ff:
        w = min(step, d_ff - c)
        out.append((c, w))
        c += w
    return out


def _ffn_kernel(h_ref, x1_ref, wup_ref, cw_ref, cb_ref, wdn_ref, gate_ref, o_ref, carry_ref, *, d_ff):
    tm = h_ref.shape[0]

    @pl.when(pl.program_id(1) == 0)
    def _():
        carry_ref[...] = jnp.zeros_like(carry_ref)

    hb = h_ref[...]

    def conv_branch(col0, width):
        u = _dot(hb, wup_ref[:, col0:col0 + width])
        prev = carry_ref[:, col0:col0 + width]
        carry_ref[:, col0:col0 + width] = u[tm - SUBLANES:tm, :]
        cw = cw_ref[:, col0:col0 + width]
        y = u * cw[FFN_CONV - 1:FFN_CONV, :] + cb_ref[:, col0:col0 + width]
        for k in range(1, FFN_CONV):
            y = y + _shift_rows(u, prev, k) * cw[FFN_CONV - 1 - k:FFN_CONV - k, :]
        return y

    acc = None
    for c0, width in _ffn_chunks(d_ff):
        gate_br = conv_branch(c0, width)
        val_br = conv_branch(d_ff + c0, width)
        act = (_silu(gate_br) * val_br).astype(BF16)
        part = _dot(act, wdn_ref[c0:c0 + width, :])
        acc = part if acc is None else acc + part
    o_ref[...] = x1_ref[...] + gate_ref[...] * acc


def _conv_ffn(h2, x1, w_up, conv_w, conv_b, w_down, gate2, *, tm):
    bsz, seq, d = x1.shape
    d_ff = w_down.shape[0]
    row = lambda b, s: (b, s, 0)
    per_b = lambda b, s: (b, 0, 0)
    const = lambda b, s: (0, 0)
    return pl.pallas_call(
        functools.partial(_ffn_kernel, d_ff=d_ff),
        out_shape=jax.ShapeDtypeStruct((bsz, seq, d), F32),
        grid=(bsz, seq // tm),
        in_specs=[pl.BlockSpec((None, tm, d), row),
                  pl.BlockSpec((None, tm, d), row),
                  pl.BlockSpec((d, 2 * d_ff), const, pipeline_mode=pl.Buffered(1)),
                  pl.BlockSpec((FFN_CONV, 2 * d_ff), const),
                  pl.BlockSpec((1, 2 * d_ff), const),
                  pl.BlockSpec((d_ff, d), const, pipeline_mode=pl.Buffered(1)),
                  pl.BlockSpec((None, 1, d), per_b)],
        out_specs=pl.BlockSpec((None, tm, d), row),
        scratch_shapes=[pltpu.VMEM((SUBLANES, 2 * d_ff), F32)],
        compiler_params=pltpu.CompilerParams(
            dimension_semantics=("parallel", "arbitrary"), vmem_limit_bytes=VMEM_LIMIT),
        name="conv_ffn",
    )(h2, x1, w_up, conv_w, conv_b.reshape(1, 2 * d_ff), w_down, gate2)


def _pad_lanes(v, fill=0.0):
    n = v.shape[-1]
    return jnp.pad(v, (0, LANES - n), constant_values=fill).reshape(1, LANES)


def kernel(x, c, w_ada, b_ada, norm1_g, w_in, q_norm_g, k_norm_g, conv_ssm_w, conv_ssm_b, dt_bias, a_log, d_skip, ssm_norm_g, attn_norm_g, w_out, norm2_g, w_up, conv_ffn_w, conv_ffn_b, w_down):
    depth = w_ada.shape[0]
    bsz, seq, d = x.shape
    n_heads_ssm = dt_bias.shape[-1]
    d_ssm = ssm_norm_g.shape[-1]
    d_attn = attn_norm_g.shape[-1]
    d_xbc = conv_ssm_w.shape[-1]
    d_in = w_in.shape[-1]
    assert d_in == 3 * d_attn + d_ssm + d_xbc + n_heads_ssm
    assert seq % MOBA_BLOCK == 0 and seq % SSM_CHUNK == 0
    assert d_ssm == n_heads_ssm * SSM_HEAD_DIM and n_heads_ssm <= LANES
    for l in range(depth):
        mod = _adaln_mod(c, w_ada[l], b_ada[l])
        shift1, scale1, gate1, shift2, scale2, gate2 = (
            m.reshape(bsz, 1, d) for m in jnp.split(mod, 6, axis=-1))
        w_in_p = jnp.pad(w_in[l].astype(BF16), ((0, 0), (0, LANES - n_heads_ssm)))
        q, k, v, z, xbc, dt = _inproj(x, norm1_g[l], scale1, shift1, w_in_p, q_norm_g[l], k_norm_g[l],
                                      d_attn=d_attn, d_ssm=d_ssm, d_xbc=d_xbc, tm=512)
        y_attn = _moba_attention(q, k, v)
        y_ssm = _ssd(xbc, dt, z, conv_ssm_w[l], conv_ssm_b[l], _pad_lanes(dt_bias[l]), _pad_lanes(a_log[l]),
                     jnp.repeat(d_skip[l], SSM_HEAD_DIM).reshape(1, d_ssm), ssm_norm_g[l],
                     n_heads=n_heads_ssm)
        w_o = w_out[l].astype(BF16)
        x1, h2 = _outproj(y_attn, y_ssm, x, attn_norm_g[l], w_o[:d_attn], w_o[d_attn:], gate1,
                          norm2_g[l], scale2, shift2, tm=512)
        x = _conv_ffn(h2, x1, w_up[l].astype(BF16), conv_ffn_w[l], conv_ffn_b[l], w_down[l].astype(BF16),
                      gate2, tm=512)
    return x
```

```python
import functools
import math

import jax
import jax.numpy as jnp
from jax import lax
from jax.experimental import pallas as pl
from jax.experimental.pallas import tpu as pltpu

F32 = jnp.float32
BF16 = jnp.bfloat16

EPS = 1e-6
LOG2E = math.log2(math.e)
LANES = 128
SUBLANES = 8
BF16_ROWS = 16
VMEM_LIMIT = 56 * 1024 * 1024

ATTN_HEAD_DIM = 128
MOBA_BLOCK = 256
MOBA_TOPK = 3
SSM_HEAD_DIM = 64
SSM_GROUPS = 2
SSM_STATE = 128
SSM_CONV = 4
SSM_CHUNK = 1024
SSM_SUB = 128
FFN_CONV = 3
MASK_BIAS = -(2.0 ** 100)

NT_DIMS = (((1,), (1,)), ((), ()))
TN_DIMS = (((0,), (0,)), ((), ()))


def _dot(a, b):
    return jnp.dot(a, b, preferred_element_type=F32)


def _dot_nt(a, b):
    return lax.dot_general(a, b, NT_DIMS, preferred_element_type=F32)


def _split3(a):
    hi = a.astype(BF16)
    r1 = a - hi.astype(F32)
    mid = r1.astype(BF16)
    lo = (r1 - mid.astype(F32)).astype(BF16)
    return hi, mid, lo


def _dot3_lhs(a, b_bf16):
    hi, mid, lo = _split3(a)
    return _dot(hi, b_bf16) + _dot(mid, b_bf16) + _dot(lo, b_bf16)


def _dot3_rhs(a_bf16, b):
    hi, mid, lo = _split3(b)
    return _dot(a_bf16, hi) + _dot(a_bf16, mid) + _dot(a_bf16, lo)


def _silu(x):
    return x * (1.0 / (1.0 + jnp.exp(-x)))


def _causal_conv(ext_ref, cur, cw, bias, cols, width):
    tm = cur.shape[0]
    ext_ref[SUBLANES:SUBLANES + tm, cols] = cur
    y = cur * cw[width - 1:width, :] + bias
    for k in range(1, width):
        y = y + ext_ref[pl.ds(SUBLANES - k, tm), cols] * cw[width - 1 - k:width - k, :]
    ext_ref[0:SUBLANES, cols] = cur[tm - SUBLANES:tm, :]
    return y


def _mod_kernel(c_ref, w_ref, b_ref, o_ref):
    sc = _silu(c_ref[...])
    o_ref[...] = jnp.dot(sc, w_ref[...], precision=lax.Precision.HIGHEST,
                         preferred_element_type=F32) + b_ref[...]


def _adaln_mod(c, w_ada, b_ada):
    bsz, d = c.shape
    n = w_ada.shape[1]
    tn = 1536
    return pl.pallas_call(
        _mod_kernel,
        out_shape=jax.ShapeDtypeStruct((bsz, n), F32),
        grid=(n // tn,),
        in_specs=[pl.BlockSpec((bsz, d), lambda j: (0, 0)),
                  pl.BlockSpec((d, tn), lambda j: (0, j)),
                  pl.BlockSpec((1, tn), lambda j: (0, j))],
        out_specs=pl.BlockSpec((bsz, tn), lambda j: (0, j)),
        name="adaln_mod",
    )(c, w_ada, b_ada.reshape(1, n))


def _inproj_kernel(x_ref, g1_ref, sc_ref, sh_ref, w_ref, gq_ref, gk_ref, cw_ref, cb_ref,
                   q_ref, k_ref, v_ref, z_ref, xbc_ref, dt_ref, ext_ref, *, d_attn, d_ssm, d_xbc):
    @pl.when(pl.program_id(1) == 0)
    def _():
        ext_ref[0:SUBLANES, :] = jnp.zeros((SUBLANES, d_xbc), F32)

    x = x_ref[...]
    ms = jnp.mean(x * x, axis=-1, keepdims=True)
    h = x * lax.rsqrt(ms + EPS) * g1_ref[...]
    h = h * (1.0 + sc_ref[...]) + sh_ref[...]
    hb = h.astype(BF16)
    n_heads = d_attn // ATTN_HEAD_DIM

    def proj(c0, c1):
        return _dot(hb, w_ref[:, c0:c1])

    def head_norm(p_all, g, out_ref):
        for hd in range(n_heads):
            lo = hd * ATTN_HEAD_DIM
            p = p_all[:, lo:lo + ATTN_HEAD_DIM]
            r = lax.rsqrt(jnp.mean(p * p, axis=-1, keepdims=True) + EPS)
            out_ref[:, lo:lo + ATTN_HEAD_DIM] = (p * r * g).astype(BF16)

    o2, o3 = 2 * d_attn, 3 * d_attn
    o4 = o3 + d_ssm
    o5 = o4 + d_xbc
    cw = cw_ref[...]
    step = 4 * LANES
    conv_cols = [slice(c0, c0 + step) for c0 in range(0, d_xbc, step)]

    def conv_pre(cols):
        return proj(o4 + cols.start, o4 + cols.stop)

    def conv_post(pre, cols):
        y = _causal_conv(ext_ref, pre, cw[:, cols], cb_ref[:, cols], cols, SSM_CONV)
        xbc_ref[:, cols] = _silu(y).astype(BF16)

    pre = conv_pre(conv_cols[0])
    q_all = proj(0, d_attn)
    conv_post(pre, conv_cols[0])
    pre = conv_pre(conv_cols[1])
    head_norm(q_all, gq_ref[...] * (ATTN_HEAD_DIM ** -0.5 * LOG2E), q_ref)
    k_all = proj(d_attn, o2)
    conv_post(pre, conv_cols[1])
    pre = conv_pre(conv_cols[2])
    head_norm(k_all, gk_ref[...], k_ref)
    v_ref[...] = proj(o2, o3).astype(BF16)
    conv_post(pre, conv_cols[2])
    for extra in conv_cols[3:]:
        conv_post(conv_pre(extra), extra)
    z_ref[...] = proj(o3, o4).astype(BF16)
    dt_ref[...] = proj(o5, o5 + LANES)


def _inproj(x, norm1_g, scale1, shift1, w_in_p, q_norm_g, k_norm_g, conv_w, conv_b,
            *, d_attn, d_ssm, d_xbc, tm):
    bsz, seq, d = x.shape
    n_pad = w_in_p.shape[1]
    row = lambda b, s: (b, s, 0)
    per_b = lambda b, s: (b, 0, 0)
    const = lambda b, s: (0, 0)
    kern = functools.partial(_inproj_kernel, d_attn=d_attn, d_ssm=d_ssm, d_xbc=d_xbc)
    return pl.pallas_call(
        kern,
        out_shape=[jax.ShapeDtypeStruct((bsz, seq, d_attn), BF16),
                   jax.ShapeDtypeStruct((bsz, seq, d_attn), BF16),
                   jax.ShapeDtypeStruct((bsz, seq, d_attn), BF16),
                   jax.ShapeDtypeStruct((bsz, seq, d_ssm), BF16),
                   jax.ShapeDtypeStruct((bsz, seq, d_xbc), BF16),
                   jax.ShapeDtypeStruct((bsz, seq, LANES), F32)],
        grid=(bsz, seq // tm),
        in_specs=[pl.BlockSpec((None, tm, d), row),
                  pl.BlockSpec((1, d), const),
                  pl.BlockSpec((None, 1, d), per_b),
                  pl.BlockSpec((None, 1, d), per_b),
                  pl.BlockSpec((d, n_pad), const, pipeline_mode=pl.Buffered(1)),
                  pl.BlockSpec((1, ATTN_HEAD_DIM), const),
                  pl.BlockSpec((1, ATTN_HEAD_DIM), const),
                  pl.BlockSpec((SSM_CONV, d_xbc), const),
                  pl.BlockSpec((1, d_xbc), const)],
        out_specs=[pl.BlockSpec((None, tm, d_attn), row),
                   pl.BlockSpec((None, tm, d_attn), row),
                   pl.BlockSpec((None, tm, d_attn), row),
                   pl.BlockSpec((None, tm, d_ssm), row),
                   pl.BlockSpec((None, tm, d_xbc), row),
                   pl.BlockSpec((None, tm, LANES), row)],
        scratch_shapes=[pltpu.VMEM((tm + SUBLANES, d_xbc), F32)],
        compiler_params=pltpu.CompilerParams(
            dimension_semantics=("parallel", "arbitrary"), vmem_limit_bytes=VMEM_LIMIT),
        name="inproj",
    )(x, norm1_g.reshape(1, d), scale1, shift1, w_in_p,
      q_norm_g.reshape(1, ATTN_HEAD_DIM), k_norm_g.reshape(1, ATTN_HEAD_DIM),
      conv_w, conv_b.reshape(1, d_xbc))


def _attn_kernel(q_ref, k_ref, v_ref, o_ref, kaug_ref, *, seq):
    blk = MOBA_BLOCK
    nb = seq // blk
    dh = ATTN_HEAD_DIM
    neg_inf = float("-inf")

    k_all = k_ref[...]
    pool_n = lax.broadcasted_iota(jnp.int32, (nb, seq), 0)
    pool_s = lax.broadcasted_iota(jnp.int32, (nb, seq), 1)
    lo = pool_n * blk
    pool = jnp.where((pool_s >= lo) & (pool_s < lo + blk), 1.0 / blk, 0.0).astype(BF16)
    k_mean = _dot(pool, k_all)

    oh_lane = lax.broadcasted_iota(jnp.int32, (seq, LANES), 1)
    oh_row = lax.broadcasted_iota(jnp.int32, (seq, LANES), 0)
    oh_lo = oh_lane * blk
    kaug_ref[:, 0:dh] = k_all
    kaug_ref[:, dh:dh + LANES] = jnp.where((oh_row >= oh_lo) & (oh_row < oh_lo + blk), 1.0, 0.0).astype(BF16)

    one_row = lax.broadcasted_iota(jnp.int32, (BF16_ROWS, seq), 0) == 0
    v_aug = jnp.concatenate([v_ref[...].T, jnp.where(one_row, 1.0, 0.0).astype(BF16)], axis=0)

    key_pos = lax.broadcasted_iota(jnp.int32, (blk, blk), 0)
    qry_pos = lax.broadcasted_iota(jnp.int32, (blk, blk), 1)
    causal = key_pos <= qry_pos
    blk_id = lax.broadcasted_iota(jnp.int32, (nb, blk), 0)

    def query_block(i):
        qi = q_ref[i * blk:(i + 1) * blk, :]
        if i <= MOBA_TOPK:
            return qi
        gate = lax.dot_general(k_mean, qi.astype(F32), NT_DIMS,
                               precision=lax.Precision.HIGHEST, preferred_element_type=F32)
        past = blk_id < i
        rows = []
        for j in range(i):
            gj = gate[j:j + 1, :]
            beats = ((gate > gj) | ((gate == gj) & (blk_id < j))) & past
            cnt = jnp.sum(beats.astype(F32), axis=0, keepdims=True)
            rows.append(jnp.where(cnt < float(MOBA_TOPK), 0.0, MASK_BIAS))
        rows.append(jnp.zeros((LANES - i, blk), F32))
        bias = jnp.concatenate(rows, axis=0).T.astype(BF16)
        return jnp.concatenate([qi, bias], axis=-1)

    def scores(i, qx):
        keys = k_ref if i <= MOBA_TOPK else kaug_ref
        tiles = []
        for j in range(i + 1):
            s = _dot_nt(keys[j * blk:(j + 1) * blk, :], qx)
            if j == i:
                s = jnp.where(causal, s, neg_inf)
            tiles.append(s)
        return tiles

    q_blocks = [query_block(i) for i in range(nb)]
    nxt = scores(0, q_blocks[0])
    for i in range(nb):
        tiles = nxt
        if i + 1 < nb:
            nxt = scores(i + 1, q_blocks[i + 1])
        m = tiles[0]
        for s in tiles[1:]:
            m = jnp.maximum(m, s)
        m = jnp.max(m, axis=0, keepdims=True)
        acc = jnp.zeros((dh + BF16_ROWS, blk), F32)
        for j, s in enumerate(tiles):
            p = jnp.exp2(s - m).astype(BF16)
            acc = acc + _dot(v_aug[:, j * blk:(j + 1) * blk], p)
        out_t = acc[0:dh, :] * (1.0 / acc[dh:dh + 1, :])
        o_ref[i * blk:(i + 1) * blk, :] = out_t.T.astype(BF16)


def _moba_attention(q, k, v):
    bsz, seq, d_attn = q.shape
    n_heads = d_attn // ATTN_HEAD_DIM
    spec = pl.BlockSpec((None, seq, ATTN_HEAD_DIM), lambda b, h: (b, 0, h))
    return pl.pallas_call(
        functools.partial(_attn_kernel, seq=seq),
        out_shape=jax.ShapeDtypeStruct((bsz, seq, d_attn), BF16),
        grid=(bsz, n_heads),
        in_specs=[spec, spec, spec],
        out_specs=spec,
        scratch_shapes=[pltpu.VMEM((seq, ATTN_HEAD_DIM + LANES), BF16)],
        compiler_params=pltpu.CompilerParams(
            dimension_semantics=("parallel", "parallel"), vmem_limit_bytes=VMEM_LIMIT),
        name="moba_attn",
    )(q, k, v)


def _ssd_kernel(xbc_ref, dt_ref, z_ref, dtb_ref, alog_ref, dskip_ref, gn_ref,
                y_ref, state_ref, *, d_ssm, n_heads):
    cl = SSM_SUB
    gs = SSM_GROUPS * SSM_STATE
    hpg = n_heads // SSM_GROUPS
    gw = d_ssm // SSM_GROUPS
    heads_per_tile = LANES // SSM_HEAD_DIM

    @pl.when(pl.program_id(1) == 0)
    def _():
        state_ref[...] = jnp.zeros_like(state_ref)

    r_io = lax.broadcasted_iota(jnp.int32, (cl, cl), 0)
    c_io = lax.broadcasted_iota(jnp.int32, (cl, cl), 1)
    tril = r_io >= c_io
    tril_b = tril.astype(BF16)
    e_h = lax.broadcasted_iota(jnp.int32, (LANES, d_ssm), 0)
    e_c = lax.broadcasted_iota(jnp.int32, (LANES, d_ssm), 1)
    c_lo = e_h * SSM_HEAD_DIM
    expand = ((e_c >= c_lo) & (e_c < c_lo + SSM_HEAD_DIM)).astype(BF16)
    lower_half = lax.broadcasted_iota(jnp.int32, (cl, LANES), 1) < SSM_HEAD_DIM
    neg_a = -jnp.exp(alog_ref[...])
    dtb = dtb_ref[...]

    n_sub = SSM_CHUNK // cl
    groups = range(SSM_GROUPS)
    row_sl = [slice(c * cl, (c + 1) * cl) for c in range(n_sub)]
    grp_sl = [slice(g * gw, (g + 1) * gw) for g in groups]

    def b_of(c, g):
        return xbc_ref[row_sl[c], d_ssm + g * SSM_STATE:d_ssm + (g + 1) * SSM_STATE]

    def c_of(c, g):
        return xbc_ref[row_sl[c], d_ssm + gs + g * SSM_STATE:d_ssm + gs + (g + 1) * SSM_STATE]

    def prepare(c):
        dtr = dt_ref[row_sl[c], :] + dtb
        dt = jnp.maximum(dtr, 0.0) + jnp.log(1.0 + jnp.exp(-jnp.abs(dtr)))
        a_cum = _dot3_rhs(tril_b, dt * neg_a)
        a2 = a_cum * LOG2E
        e_cum = jnp.exp(a_cum)
        w_dt = jnp.exp(a_cum[cl - 1:cl, :] - a_cum) * dt
        return dict(a2=a2, a2_t=a2.T, dt_t=dt.T,
                    e_x=_dot3_lhs(e_cum, expand), w_x=_dot3_lhs(w_dt, expand))

    def intra(c, g, pre):
        a2, a2_t, dt_t = pre["a2"], pre["a2_t"], pre["dt_t"]
        cb = jnp.where(tril, _dot_nt(c_of(c, g), b_of(c, g)), 0.0)
        tiles = []
        for t in range(gw // LANES):
            x_tile = xbc_ref[row_sl[c], g * gw + t * LANES:g * gw + (t + 1) * LANES]
            parts = []
            for half in range(heads_per_tile):
                hd = g * hpg + t * heads_per_tile + half
                diff = jnp.minimum(a2[:, hd:hd + 1] - a2_t[hd:hd + 1, :], 0.0)
                mh = cb * jnp.exp2(diff) * dt_t[hd:hd + 1, :]
                parts.append(_dot(mh.astype(BF16), x_tile))
            tiles.append(jnp.where(lower_half, parts[0], parts[1]))
        return jnp.concatenate(tiles, axis=-1)

    state = [state_ref[g] for g in groups]
    nxt = prepare(0)
    for c in range(n_sub):
        pre = nxt
        if c + 1 < n_sub:
            nxt = prepare(c + 1)
        for g in groups:
            gsl = grp_sl[g]
            y_diag = intra(c, g, pre)
            xs_g = xbc_ref[row_sl[c], gsl].astype(F32)
            e_g = pre["e_x"][:, gsl]
            y_off = _dot(c_of(c, g), state[g].astype(BF16)) * e_g
            xw = (xs_g * pre["w_x"][:, gsl]).astype(BF16)
            s_new = lax.dot_general(b_of(c, g), xw, TN_DIMS, preferred_element_type=F32)
            state[g] = state[g] * e_g[cl - 1:cl, :] + s_new
            yg = y_diag + y_off + xs_g * dskip_ref[:, gsl]
            yg = yg * _silu(z_ref[row_sl[c], gsl].astype(F32))
            r = lax.rsqrt(jnp.mean(yg * yg, axis=-1, keepdims=True) + EPS)
            y_ref[row_sl[c], gsl] = (yg * r * gn_ref[:, gsl]).astype(BF16)
    for g in groups:
        state_ref[g] = state[g]


def _ssd(xbc, dt, z, dt_bias_p, a_log_p, d_skip_x, ssm_norm_g, *, n_heads):
    bsz, seq, d_xbc = xbc.shape
    d_ssm = z.shape[-1]
    cl = SSM_CHUNK
    row = lambda b, c: (b, c, 0)
    const = lambda b, c: (0, 0)
    kern = functools.partial(_ssd_kernel, d_ssm=d_ssm, n_heads=n_heads)
    return pl.pallas_call(
        kern,
        out_shape=jax.ShapeDtypeStruct((bsz, seq, d_ssm), BF16),
        grid=(bsz, seq // cl),
        in_specs=[pl.BlockSpec((None, cl, d_xbc), row),
                  pl.BlockSpec((None, cl, LANES), row),
                  pl.BlockSpec((None, cl, d_ssm), row),
                  pl.BlockSpec((1, LANES), const),
                  pl.BlockSpec((1, LANES), const),
                  pl.BlockSpec((1, d_ssm), const),
                  pl.BlockSpec((1, d_ssm), const)],
        out_specs=pl.BlockSpec((None, cl, d_ssm), row),
        scratch_shapes=[pltpu.VMEM((SSM_GROUPS, SSM_STATE, d_ssm // SSM_GROUPS), F32)],
        compiler_params=pltpu.CompilerParams(
            dimension_semantics=("parallel", "arbitrary"), vmem_limit_bytes=VMEM_LIMIT),
        name="ssd",
    )(xbc, dt, z, dt_bias_p, a_log_p, d_skip_x, ssm_norm_g.reshape(1, d_ssm))


def _outproj_kernel(ya_ref, ys_ref, x_ref, ga_ref, wa_ref, ws_ref, gate_ref, g2_ref, sc_ref, sh_ref,
                    x1_ref, h2_ref):
    ya = ya_ref[...].astype(F32)
    r = lax.rsqrt(jnp.mean(ya * ya, axis=-1, keepdims=True) + EPS)
    ya = (ya * r * ga_ref[...]).astype(BF16)
    mix = _dot(ya, wa_ref[...]) + _dot(ys_ref[...], ws_ref[...])
    x1 = x_ref[...] + gate_ref[...] * mix
    x1_ref[...] = x1
    ms = jnp.mean(x1 * x1, axis=-1, keepdims=True)
    h = x1 * lax.rsqrt(ms + EPS) * g2_ref[...]
    h2_ref[...] = (h * (1.0 + sc_ref[...]) + sh_ref[...]).astype(BF16)


def _outproj(y_attn, y_ssm, x, attn_norm_g, w_out_a, w_out_s, gate1, norm2_g, scale2, shift2, *, tm):
    bsz, seq, d = x.shape
    d_attn = y_attn.shape[-1]
    d_ssm = y_ssm.shape[-1]
    row = lambda b, s: (b, s, 0)
    per_b = lambda b, s: (b, 0, 0)
    const = lambda b, s: (0, 0)
    return pl.pallas_call(
        _outproj_kernel,
        out_shape=[jax.ShapeDtypeStruct((bsz, seq, d), F32),
                   jax.ShapeDtypeStruct((bsz, seq, d), BF16)],
        grid=(bsz, seq // tm),
        in_specs=[pl.BlockSpec((None, tm, d_attn), row),
                  pl.BlockSpec((None, tm, d_ssm), row),
                  pl.BlockSpec((None, tm, d), row),
                  pl.BlockSpec((1, d_attn), const),
                  pl.BlockSpec((d_attn, d), const),
                  pl.BlockSpec((d_ssm, d), const),
                  pl.BlockSpec((None, 1, d), per_b),
                  pl.BlockSpec((1, d), const),
                  pl.BlockSpec((None, 1, d), per_b),
                  pl.BlockSpec((None, 1, d), per_b)],
        out_specs=[pl.BlockSpec((None, tm, d), row),
                   pl.BlockSpec((None, tm, d), row)],
        compiler_params=pltpu.CompilerParams(
            dimension_semantics=("parallel", "parallel"), vmem_limit_bytes=VMEM_LIMIT),
        name="outproj",
    )(y_attn, y_ssm, x, attn_norm_g.reshape(1, d_attn), w_out_a, w_out_s, gate1,
      norm2_g.reshape(1, d), scale2, shift2)


def _ffn_chunks(d_ff):
    step = 768
    out, c = [], 0
    while c < d_ff:
        w = min(step, d_ff - c)
        out.append((c, w))
        c += w
    return out


def _ffn_kernel(h_ref, x1_ref, wup_ref, cw_ref, cb_ref, wdn_ref, gate_ref, o_ref, ext_ref, *, d_ff):
    @pl.when(pl.program_id(1) == 0)
    def _():
        ext_ref[0:SUBLANES, :] = jnp.zeros((SUBLANES, 2 * d_ff), F32)

    def up(c0, width):
        return (_dot(h_ref[...], wup_ref[:, c0:c0 + width]),
                _dot(h_ref[...], wup_ref[:, d_ff + c0:d_ff + c0 + width]))

    def conv(u, col0, width):
        cols = slice(col0, col0 + width)
        return _causal_conv(ext_ref, u, cw_ref[:, cols], cb_ref[:, cols], cols, FFN_CONV)

    chunks = _ffn_chunks(d_ff)
    acc = None
    nxt = up(*chunks[0])
    for idx, (c0, width) in enumerate(chunks):
        u_gate, u_val = nxt
        if idx + 1 < len(chunks):
            nxt = up(*chunks[idx + 1])
        act = (_silu(conv(u_gate, c0, width)) * conv(u_val, d_ff + c0, width)).astype(BF16)
        part = _dot(act, wdn_ref[c0:c0 + width, :])
        acc = part if acc is None else acc + part
    o_ref[...] = x1_ref[...] + gate_ref[...] * acc


def _conv_ffn(h2, x1, w_up, conv_w, conv_b, w_down, gate2, *, tm):
    bsz, seq, d = x1.shape
    d_ff = w_down.shape[0]
    row = lambda b, s: (b, s, 0)
    per_b = lambda b, s: (b, 0, 0)
    const = lambda b, s: (0, 0)
    return pl.pallas_call(
        functools.partial(_ffn_kernel, d_ff=d_ff),
        out_shape=jax.ShapeDtypeStruct((bsz, seq, d), F32),
        grid=(bsz, seq // tm),
        in_specs=[pl.BlockSpec((None, tm, d), row),
                  pl.BlockSpec((None, tm, d), row),
                  pl.BlockSpec((d, 2 * d_ff), const, pipeline_mode=pl.Buffered(1)),
                  pl.BlockSpec((FFN_CONV, 2 * d_ff), const),
                  pl.BlockSpec((1, 2 * d_ff), const),
                  pl.BlockSpec((d_ff, d), const, pipeline_mode=pl.Buffered(1)),
                  pl.BlockSpec((None, 1, d), per_b)],
        out_specs=pl.BlockSpec((None, tm, d), row),
        scratch_shapes=[pltpu.VMEM((tm + SUBLANES, 2 * d_ff), F32)],
        compiler_params=pltpu.CompilerParams(
            dimension_semantics=("parallel", "arbitrary"), vmem_limit_bytes=VMEM_LIMIT),
        name="conv_ffn",
    )(h2, x1, w_up, conv_w, conv_b.reshape(1, 2 * d_ff), w_down, gate2)


def _pad_lanes(v, fill=0.0):
    n = v.shape[-1]
    return jnp.pad(v, (0, LANES - n), constant_values=fill).reshape(1, LANES)


def kernel(x, c, w_ada, b_ada, norm1_g, w_in, q_norm_g, k_norm_g, conv_ssm_w, conv_ssm_b, dt_bias, a_log, d_skip, ssm_norm_g, attn_norm_g, w_out, norm2_g, w_up, conv_ffn_w, conv_ffn_b, w_down):
    depth = w_ada.shape[0]
    bsz, seq, d = x.shape
    n_heads_ssm = dt_bias.shape[-1]
    d_ssm = ssm_norm_g.shape[-1]
    d_attn = attn_norm_g.shape[-1]
    d_xbc = conv_ssm_w.shape[-1]
    d_in = w_in.shape[-1]
    assert d_in == 3 * d_attn + d_ssm + d_xbc + n_heads_ssm
    assert seq % MOBA_BLOCK == 0 and seq % SSM_CHUNK == 0
    assert d_ssm == n_heads_ssm * SSM_HEAD_DIM and n_heads_ssm <= LANES
    for l in range(depth):
        mod = _adaln_mod(c, w_ada[l], b_ada[l])
        shift1, scale1, gate1, shift2, scale2, gate2 = (
            m.reshape(bsz, 1, d) for m in jnp.split(mod, 6, axis=-1))
        w_in_p = jnp.pad(w_in[l].astype(BF16), ((0, 0), (0, LANES - n_heads_ssm)))
        q, k, v, z, xbc, dt = _inproj(x, norm1_g[l], scale1, shift1, w_in_p, q_norm_g[l], k_norm_g[l],
                                      conv_ssm_w[l], conv_ssm_b[l],
                                      d_attn=d_attn, d_ssm=d_ssm, d_xbc=d_xbc, tm=512)
        y_attn = _moba_attention(q, k, v)
        y_ssm = _ssd(xbc, dt, z, _pad_lanes(dt_bias[l]), _pad_lanes(a_log[l]),
                     jnp.repeat(d_skip[l], SSM_HEAD_DIM).reshape(1, d_ssm), ssm_norm_g[l],
                     n_heads=n_heads_ssm)
        w_o = w_out[l].astype(BF16)
        x1, h2 = _outproj(y_attn, y_ssm, x, attn_norm_g[l], w_o[:d_attn], w_o[d_attn:], gate1,
                          norm2_g[l], scale2, shift2, tm=512)
        x = _conv_ffn(h2, x1, w_up[l].astype(BF16), conv_ffn_w[l], conv_ffn_b[l], w_down[l].astype(BF16),
                      gate2, tm=512)
    return x
```

```python
import functools
import math

import jax
import jax.numpy as jnp
from jax import lax
from jax.experimental import pallas as pl
from jax.experimental.pallas import tpu as pltpu

F32 = jnp.float32
BF16 = jnp.bfloat16

EPS = 1e-6
LOG2E = math.log2(math.e)
LANES = 128
SUBLANES = 8
BF16_ROWS = 16
VMEM_LIMIT = 56 * 1024 * 1024

ATTN_HEAD_DIM = 128
MOBA_BLOCK = 256
MOBA_TOPK = 3
SSM_HEAD_DIM = 64
SSM_GROUPS = 2
SSM_STATE = 128
SSM_CONV = 4
SSM_CHUNK = 1024
SSM_SUB = 128
FFN_CONV = 3
EXP2_CLAMP = 127.0

NT_DIMS = (((1,), (1,)), ((), ()))
TN_DIMS = (((0,), (0,)), ((), ()))


def _dot(a, b):
    return jnp.dot(a, b, preferred_element_type=F32)


def _dot_nt(a, b):
    return lax.dot_general(a, b, NT_DIMS, preferred_element_type=F32)


def _split3(a):
    hi = a.astype(BF16)
    r1 = a - hi.astype(F32)
    mid = r1.astype(BF16)
    lo = (r1 - mid.astype(F32)).astype(BF16)
    return hi, mid, lo


def _dot3_lhs(a, b_bf16):
    hi, mid, lo = _split3(a)
    return _dot(hi, b_bf16) + _dot(mid, b_bf16) + _dot(lo, b_bf16)


def _dot2_lhs(a, b_bf16):
    hi, mid, _ = _split3(a)
    return _dot(hi, b_bf16) + _dot(mid, b_bf16)


def _dot3_rhs(a_bf16, b):
    hi, mid, lo = _split3(b)
    return _dot(a_bf16, hi) + _dot(a_bf16, mid) + _dot(a_bf16, lo)


def _silu(x):
    t = 0.5 * x
    return t + t * jnp.tanh(t)


def _causal_conv(ext_ref, cur, cw, bias, cols, width):
    tm = cur.shape[0]
    ext_ref[SUBLANES:SUBLANES + tm, cols] = cur
    y = cur * cw[width - 1:width, :] + bias
    for k in range(1, width):
        y = y + ext_ref[pl.ds(SUBLANES - k, tm), cols] * cw[width - 1 - k:width - k, :]
    ext_ref[0:SUBLANES, cols] = cur[tm - SUBLANES:tm, :]
    return y


def _mod_kernel(c_ref, w_ref, b_ref, o_ref):
    sc = _silu(c_ref[...])
    o_ref[...] = jnp.dot(sc, w_ref[...], precision=lax.Precision.HIGHEST,
                         preferred_element_type=F32) + b_ref[...]


def _adaln_mod(c, w_ada, b_ada):
    bsz, d = c.shape
    n = w_ada.shape[1]
    tn = 1536
    return pl.pallas_call(
        _mod_kernel,
        out_shape=jax.ShapeDtypeStruct((bsz, n), F32),
        grid=(n // tn,),
        in_specs=[pl.BlockSpec((bsz, d), lambda j: (0, 0)),
                  pl.BlockSpec((d, tn), lambda j: (0, j)),
                  pl.BlockSpec((1, tn), lambda j: (0, j))],
        out_specs=pl.BlockSpec((bsz, tn), lambda j: (0, j)),
        name="adaln_mod",
    )(c, w_ada, b_ada.reshape(1, n))


def _inproj_kernel(x_ref, g1_ref, sc_ref, sh_ref, w_ref, wdt_ref, gq_ref, gk_ref, cw_ref, cb_ref,
                   q_ref, k_ref, vt_ref, z_ref, xbc_ref, dt_ref, kmean_ref, ext_ref,
                   *, d_attn, d_ssm, d_xbc):
    tm = x_ref.shape[0]

    @pl.when(pl.program_id(1) == 0)
    def _():
        ext_ref[0:SUBLANES, :] = jnp.zeros((SUBLANES, d_xbc), F32)

    x = x_ref[...]
    ms = jnp.mean(x * x, axis=-1, keepdims=True)
    h = x * lax.rsqrt(ms + EPS) * g1_ref[...]
    h = h * (1.0 + sc_ref[...]) + sh_ref[...]
    hb = h.astype(BF16)
    n_heads = d_attn // ATTN_HEAD_DIM
    blocks_per_tile = tm // MOBA_BLOCK

    def proj(c0, c1):
        return _dot(hb, w_ref[:, c0:c1])

    def head_norm(p_all, g, out_ref, mean_ref=None):
        for hd in range(n_heads):
            lo = hd * ATTN_HEAD_DIM
            p = p_all[:, lo:lo + ATTN_HEAD_DIM]
            r = lax.rsqrt(jnp.mean(p * p, axis=-1, keepdims=True) + EPS)
            pn = p * r * g
            out_ref[:, lo:lo + ATTN_HEAD_DIM] = pn.astype(BF16)
            if mean_ref is not None:
                means = [jnp.sum(pn[blk * MOBA_BLOCK:(blk + 1) * MOBA_BLOCK, :], axis=0, keepdims=True)
                         * (1.0 / MOBA_BLOCK) for blk in range(blocks_per_tile)]
                means.append(jnp.zeros((SUBLANES - blocks_per_tile, ATTN_HEAD_DIM), F32))
                mean_ref[:, lo:lo + ATTN_HEAD_DIM] = jnp.concatenate(means, axis=0)

    o2, o3 = 2 * d_attn, 3 * d_attn
    o4 = o3 + d_ssm
    cw = cw_ref[...]
    step = 4 * LANES
    conv_cols = [slice(c0, c0 + step) for c0 in range(0, d_xbc, step)]

    def conv_pre(cols):
        return proj(o4 + cols.start, o4 + cols.stop)

    def conv_post(pre, cols):
        y = _causal_conv(ext_ref, pre, cw[:, cols], cb_ref[:, cols], cols, SSM_CONV)
        xbc_ref[:, cols] = _silu(y).astype(BF16)

    pre = conv_pre(conv_cols[0])
    q_all = proj(0, d_attn)
    conv_post(pre, conv_cols[0])
    pre = conv_pre(conv_cols[1])
    head_norm(q_all, gq_ref[...] * (ATTN_HEAD_DIM ** -0.5 * LOG2E), q_ref)
    k_all = proj(d_attn, o2)
    conv_post(pre, conv_cols[1])
    pre = conv_pre(conv_cols[2])
    head_norm(k_all, gk_ref[...], k_ref, kmean_ref)
    v_all = proj(o2, o3)
    for hd in range(n_heads):
        lo = hd * ATTN_HEAD_DIM
        vt_ref[lo:lo + ATTN_HEAD_DIM, :] = v_all[:, lo:lo + ATTN_HEAD_DIM].T.astype(BF16)
    conv_post(pre, conv_cols[2])
    for extra in conv_cols[3:]:
        conv_post(conv_pre(extra), extra)
    z_ref[...] = proj(o3, o4).astype(BF16)
    dt_ref[...] = _dot(hb, wdt_ref[...])


def _inproj(x, norm1_g, scale1, shift1, w_main, w_dt, q_norm_g, k_norm_g, conv_w, conv_b,
            *, d_attn, d_ssm, d_xbc, tm):
    bsz, seq, d = x.shape
    n_main = w_main.shape[1]
    assert tm % MOBA_BLOCK == 0 and seq % tm == 0
    row = lambda b, s: (b, s, 0)
    per_b = lambda b, s: (b, 0, 0)
    const = lambda b, s: (0, 0)
    kern = functools.partial(_inproj_kernel, d_attn=d_attn, d_ssm=d_ssm, d_xbc=d_xbc)
    *outs, kmean_tiles = pl.pallas_call(
        kern,
        out_shape=[jax.ShapeDtypeStruct((bsz, seq, d_attn), BF16),
                   jax.ShapeDtypeStruct((bsz, seq, d_attn), BF16),
                   jax.ShapeDtypeStruct((bsz, d_attn, seq), BF16),
                   jax.ShapeDtypeStruct((bsz, seq, d_ssm), BF16),
                   jax.ShapeDtypeStruct((bsz, seq, d_xbc), BF16),
                   jax.ShapeDtypeStruct((bsz, seq, LANES), F32),
                   jax.ShapeDtypeStruct((bsz, seq // tm, SUBLANES, d_attn), F32)],
        grid=(bsz, seq // tm),
        in_specs=[pl.BlockSpec((None, tm, d), row),
                  pl.BlockSpec((1, d), const),
                  pl.BlockSpec((None, 1, d), per_b),
                  pl.BlockSpec((None, 1, d), per_b),
                  pl.BlockSpec((d, n_main), const, pipeline_mode=pl.Buffered(1)),
                  pl.BlockSpec((d, LANES), const),
                  pl.BlockSpec((1, ATTN_HEAD_DIM), const),
                  pl.BlockSpec((1, ATTN_HEAD_DIM), const),
                  pl.BlockSpec((SSM_CONV, d_xbc), const),
                  pl.BlockSpec((1, d_xbc), const)],
        out_specs=[pl.BlockSpec((None, tm, d_attn), row),
                   pl.BlockSpec((None, tm, d_attn), row),
                   pl.BlockSpec((None, d_attn, tm), lambda b, s: (b, 0, s)),
                   pl.BlockSpec((None, tm, d_ssm), row),
                   pl.BlockSpec((None, tm, d_xbc), row),
                   pl.BlockSpec((None, tm, LANES), row),
                   pl.BlockSpec((None, None, SUBLANES, d_attn), lambda b, s: (b, s, 0, 0))],
        scratch_shapes=[pltpu.VMEM((tm + SUBLANES, d_xbc), F32)],
        compiler_params=pltpu.CompilerParams(
            dimension_semantics=("parallel", "arbitrary"), vmem_limit_bytes=VMEM_LIMIT),
        name="inproj",
    )(x, norm1_g.reshape(1, d), scale1, shift1, w_main, w_dt,
      q_norm_g.reshape(1, ATTN_HEAD_DIM), k_norm_g.reshape(1, ATTN_HEAD_DIM),
      conv_w, conv_b.reshape(1, d_xbc))
    kmean = kmean_tiles[:, :, :tm // MOBA_BLOCK, :].reshape(bsz, seq // MOBA_BLOCK, d_attn)
    return (*outs, kmean)


def _attn_kernel(q_ref, k_ref, vt_ref, kmean_ref, o_ref, *, seq, heads):
    blk = MOBA_BLOCK
    nb = seq // blk
    dh = ATTN_HEAD_DIM
    neg_inf = float("-inf")
    gate_rows = 3 * nb
    pad_rows = -gate_rows % BF16_ROWS
    one_row = lax.broadcasted_iota(jnp.int32, (BF16_ROWS, seq), 0) == 0
    ones_rows = jnp.where(one_row, 1.0, 0.0).astype(BF16)

    key_pos = lax.broadcasted_iota(jnp.int32, (blk, blk), 0)
    qry_pos = lax.broadcasted_iota(jnp.int32, (blk, blk), 1)
    causal = key_pos <= qry_pos
    blk_id = lax.broadcasted_iota(jnp.int32, (nb, blk), 0)

    k0_ext, v_aug = [], []
    for hd in range(heads):
        hsl = slice(hd * dh, (hd + 1) * dh)
        k0_ext.append(jnp.concatenate([k_ref[0:blk, hsl], *_split3(kmean_ref[:, hsl]),
                                       jnp.zeros((pad_rows, dh), BF16)], axis=0))
        v_aug.append(jnp.concatenate([vt_ref[hsl, :], ones_rows], axis=0))

    def scores(hd, i):
        hsl = slice(hd * dh, (hd + 1) * dh)
        qi = q_ref[i * blk:(i + 1) * blk, hsl]
        select = i > MOBA_TOPK
        tiles = []
        for j in range(i + 1):
            if j == 0 and select:
                ext = _dot_nt(k0_ext[hd], qi)
                s = ext[0:blk, :]
                gate = ext[blk:blk + nb, :] + ext[blk + nb:blk + 2 * nb, :] + ext[blk + 2 * nb:blk + 3 * nb, :]
            else:
                s = _dot_nt(k_ref[j * blk:(j + 1) * blk, hsl], qi)
            tiles.append(s)
        if select:
            past = blk_id < i
            for j in range(i):
                gj = gate[j:j + 1, :]
                beats = ((gate > gj) | ((gate == gj) & (blk_id < j))) & past
                cnt = jnp.sum(beats.astype(F32), axis=0, keepdims=True)
                tiles[j] = jnp.where(cnt < float(MOBA_TOPK), tiles[j], neg_inf)
        tiles[i] = jnp.where(causal, tiles[i], neg_inf)
        return tiles

    items = [(hd, i) for hd in range(heads) for i in range(nb)]
    nxt = scores(*items[0])
    for pos, (hd, i) in enumerate(items):
        tiles = nxt
        if pos + 1 < len(items):
            nxt = scores(*items[pos + 1])
        m = tiles[0]
        for s in tiles[1:]:
            m = jnp.maximum(m, s)
        m = jnp.max(m, axis=0, keepdims=True)
        acc = jnp.zeros((dh + BF16_ROWS, blk), F32)
        for j, s in enumerate(tiles):
            p = jnp.exp2(s - m).astype(BF16)
            acc = acc + _dot(v_aug[hd][:, j * blk:(j + 1) * blk], p)
        out_t = acc[0:dh, :] * (1.0 / acc[dh:dh + 1, :])
        o_ref[i * blk:(i + 1) * blk, hd * dh:(hd + 1) * dh] = out_t.T.astype(BF16)


def _moba_attention(q, k, vt, kmean, *, heads_per_step):
    bsz, seq, d_attn = q.shape
    n_heads = d_attn // ATTN_HEAD_DIM
    nb = seq // MOBA_BLOCK
    width = heads_per_step * ATTN_HEAD_DIM
    assert n_heads % heads_per_step == 0
    spec = pl.BlockSpec((None, seq, width), lambda b, h: (b, 0, h))
    return pl.pallas_call(
        functools.partial(_attn_kernel, seq=seq, heads=heads_per_step),
        out_shape=jax.ShapeDtypeStruct((bsz, seq, d_attn), BF16),
        grid=(bsz, n_heads // heads_per_step),
        in_specs=[spec, spec,
                  pl.BlockSpec((None, width, seq), lambda b, h: (b, h, 0)),
                  pl.BlockSpec((None, nb, width), lambda b, h: (b, 0, h))],
        out_specs=spec,
        compiler_params=pltpu.CompilerParams(
            dimension_semantics=("parallel", "parallel"), vmem_limit_bytes=VMEM_LIMIT),
        name="moba_attn",
    )(q, k, vt, kmean)


def _ssd_kernel(xbc_ref, dt_ref, z_ref, dtb_ref, alog_ref, dskip_ref, gn_ref,
                y_ref, state_ref, *, d_ssm, n_heads):
    cl = SSM_SUB
    gs = SSM_GROUPS * SSM_STATE
    hpg = n_heads // SSM_GROUPS
    gw = d_ssm // SSM_GROUPS
    heads_per_tile = LANES // SSM_HEAD_DIM

    @pl.when(pl.program_id(1) == 0)
    def _():
        state_ref[...] = jnp.zeros_like(state_ref)

    r_io = lax.broadcasted_iota(jnp.int32, (cl, cl), 0)
    c_io = lax.broadcasted_iota(jnp.int32, (cl, cl), 1)
    tril = r_io >= c_io
    tril_b = tril.astype(BF16)
    e_h = lax.broadcasted_iota(jnp.int32, (LANES, d_ssm), 0)
    e_c = lax.broadcasted_iota(jnp.int32, (LANES, d_ssm), 1)
    c_lo = e_h * SSM_HEAD_DIM
    expand = ((e_c >= c_lo) & (e_c < c_lo + SSM_HEAD_DIM)).astype(BF16)
    lower_half = lax.broadcasted_iota(jnp.int32, (cl, LANES), 1) < SSM_HEAD_DIM
    neg_a = -jnp.exp(alog_ref[...])
    dtb = dtb_ref[...]

    n_sub = SSM_CHUNK // cl
    groups = range(SSM_GROUPS)
    row_sl = [slice(c * cl, (c + 1) * cl) for c in range(n_sub)]
    grp_sl = [slice(g * gw, (g + 1) * gw) for g in groups]

    def b_of(c, g):
        return xbc_ref[row_sl[c], d_ssm + g * SSM_STATE:d_ssm + (g + 1) * SSM_STATE]

    def c_of(c, g):
        return xbc_ref[row_sl[c], d_ssm + gs + g * SSM_STATE:d_ssm + gs + (g + 1) * SSM_STATE]

    def prepare(c):
        dtr = dt_ref[row_sl[c], :] + dtb
        dt = jnp.maximum(dtr, 0.0) + jnp.log(1.0 + jnp.exp(-jnp.abs(dtr)))
        a_cum = _dot3_rhs(tril_b, dt * neg_a)
        a2 = a_cum * LOG2E
        e_cum = jnp.exp(a_cum)
        w_dt = jnp.exp(a_cum[cl - 1:cl, :] - a_cum) * dt
        return dict(a2=a2, a2s_t=(a2 - jnp.log2(dt)).T,
                    e_x=_dot2_lhs(e_cum, expand), w_x=_dot2_lhs(w_dt, expand))

    def intra(c, g, pre):
        a2, a2s_t = pre["a2"], pre["a2s_t"]
        cb = jnp.where(tril, _dot_nt(c_of(c, g), b_of(c, g)), 0.0)
        tiles = []
        for t in range(gw // LANES):
            x_tile = xbc_ref[row_sl[c], g * gw + t * LANES:g * gw + (t + 1) * LANES]
            parts = []
            for half in range(heads_per_tile):
                hd = g * hpg + t * heads_per_tile + half
                expo = jnp.minimum(a2[:, hd:hd + 1] - a2s_t[hd:hd + 1, :], EXP2_CLAMP)
                parts.append(_dot((cb * jnp.exp2(expo)).astype(BF16), x_tile))
            tiles.append(jnp.where(lower_half, parts[0], parts[1]))
        return jnp.concatenate(tiles, axis=-1)

    state = [state_ref[g] for g in groups]
    nxt = prepare(0)
    for c in range(n_sub):
        pre = nxt
        if c + 1 < n_sub:
            nxt = prepare(c + 1)
        for g in groups:
            gsl = grp_sl[g]
            y_diag = intra(c, g, pre)
            xs_g = xbc_ref[row_sl[c], gsl].astype(F32)
            e_g = pre["e_x"][:, gsl]
            y_off = _dot(c_of(c, g), state[g].astype(BF16)) * e_g
            xw = (xs_g * pre["w_x"][:, gsl]).astype(BF16)
            s_new = lax.dot_general(b_of(c, g), xw, TN_DIMS, preferred_element_type=F32)
            state[g] = state[g] * e_g[cl - 1:cl, :] + s_new
            yg = y_diag + y_off + xs_g * dskip_ref[:, gsl]
            yg = yg * _silu(z_ref[row_sl[c], gsl].astype(F32))
            r = lax.rsqrt(jnp.mean(yg * yg, axis=-1, keepdims=True) + EPS)
            y_ref[row_sl[c], gsl] = (yg * r * gn_ref[:, gsl]).astype(BF16)
    for g in groups:
        state_ref[g] = state[g]


def _ssd(xbc, dt, z, dt_bias_p, a_log_p, d_skip_x, ssm_norm_g, *, n_heads):
    bsz, seq, d_xbc = xbc.shape
    d_ssm = z.shape[-1]
    cl = SSM_CHUNK
    row = lambda b, c: (b, c, 0)
    const = lambda b, c: (0, 0)
    kern = functools.partial(_ssd_kernel, d_ssm=d_ssm, n_heads=n_heads)
    return pl.pallas_call(
        kern,
        out_shape=jax.ShapeDtypeStruct((bsz, seq, d_ssm), BF16),
        grid=(bsz, seq // cl),
        in_specs=[pl.BlockSpec((None, cl, d_xbc), row),
                  pl.BlockSpec((None, cl, LANES), row),
                  pl.BlockSpec((None, cl, d_ssm), row),
                  pl.BlockSpec((1, LANES), const),
                  pl.BlockSpec((1, LANES), const),
                  pl.BlockSpec((1, d_ssm), const),
                  pl.BlockSpec((1, d_ssm), const)],
        out_specs=pl.BlockSpec((None, cl, d_ssm), row),
        scratch_shapes=[pltpu.VMEM((SSM_GROUPS, SSM_STATE, d_ssm // SSM_GROUPS), F32)],
        compiler_params=pltpu.CompilerParams(
            dimension_semantics=("parallel", "arbitrary"), vmem_limit_bytes=VMEM_LIMIT),
        name="ssd",
    )(xbc, dt, z, dt_bias_p, a_log_p, d_skip_x, ssm_norm_g.reshape(1, d_ssm))


def _ffn_chunks(d_ff):
    step = 768
    out, c = [], 0
    while c < d_ff:
        w = min(step, d_ff - c)
        out.append((c, w))
        c += w
    return out


def _mix_ffn_kernel(ya_ref, ys_ref, x_ref, ga_ref, wa_ref, ws_ref, gate1_ref, g2_ref, sc_ref, sh_ref,
                    wup_ref, cw_ref, cb_ref, wdn_ref, gate2_ref, o_ref, ext_ref, hb_ref, *, d_ff):
    @pl.when(pl.program_id(1) == 0)
    def _():
        ext_ref[0:SUBLANES, :] = jnp.zeros((SUBLANES, 2 * d_ff), F32)

    ya = ya_ref[...].astype(F32)
    r = lax.rsqrt(jnp.mean(ya * ya, axis=-1, keepdims=True) + EPS)
    ya = (ya * r * ga_ref[...]).astype(BF16)
    mix = _dot(ya, wa_ref[...]) + _dot(ys_ref[...], ws_ref[...])
    x1 = x_ref[...] + gate1_ref[...] * mix
    o_ref[...] = x1
    ms = jnp.mean(x1 * x1, axis=-1, keepdims=True)
    h = x1 * lax.rsqrt(ms + EPS) * g2_ref[...]
    hb_ref[...] = (h * (1.0 + sc_ref[...]) + sh_ref[...]).astype(BF16)

    def up(c0, width):
        return (_dot(hb_ref[...], wup_ref[:, c0:c0 + width]),
                _dot(hb_ref[...], wup_ref[:, d_ff + c0:d_ff + c0 + width]))

    def conv(u, col0, width):
        cols = slice(col0, col0 + width)
        return _causal_conv(ext_ref, u, cw_ref[:, cols], cb_ref[:, cols], cols, FFN_CONV)

    chunks = _ffn_chunks(d_ff)
    acc = None
    nxt = up(*chunks[0])
    for idx, (c0, width) in enumerate(chunks):
        u_gate, u_val = nxt
        if idx + 1 < len(chunks):
            nxt = up(*chunks[idx + 1])
        act = (_silu(conv(u_gate, c0, width)) * conv(u_val, d_ff + c0, width)).astype(BF16)
        part = _dot(act, wdn_ref[c0:c0 + width, :])
        acc = part if acc is None else acc + part
    o_ref[...] = o_ref[...] + gate2_ref[...] * acc


def _mix_ffn(y_attn, y_ssm, x, attn_norm_g, w_out_a, w_out_s, gate1, norm2_g, scale2, shift2,
             w_up, conv_w, conv_b, w_down, gate2, *, tm):
    bsz, seq, d = x.shape
    d_attn = y_attn.shape[-1]
    d_ssm = y_ssm.shape[-1]
    d_ff = w_down.shape[0]
    row = lambda b, s: (b, s, 0)
    per_b = lambda b, s: (b, 0, 0)
    const = lambda b, s: (0, 0)
    resident = dict(pipeline_mode=pl.Buffered(1))
    return pl.pallas_call(
        functools.partial(_mix_ffn_kernel, d_ff=d_ff),
        out_shape=jax.ShapeDtypeStruct((bsz, seq, d), F32),
        grid=(bsz, seq // tm),
        in_specs=[pl.BlockSpec((None, tm, d_attn), row),
                  pl.BlockSpec((None, tm, d_ssm), row),
                  pl.BlockSpec((None, tm, d), row),
                  pl.BlockSpec((1, d_attn), const),
                  pl.BlockSpec((d_attn, d), const, **resident),
                  pl.BlockSpec((d_ssm, d), const, **resident),
                  pl.BlockSpec((None, 1, d), per_b),
                  pl.BlockSpec((1, d), const),
                  pl.BlockSpec((None, 1, d), per_b),
                  pl.BlockSpec((None, 1, d), per_b),
                  pl.BlockSpec((d, 2 * d_ff), const, **resident),
                  pl.BlockSpec((FFN_CONV, 2 * d_ff), const),
                  pl.BlockSpec((1, 2 * d_ff), const),
                  pl.BlockSpec((d_ff, d), const, **resident),
                  pl.BlockSpec((None, 1, d), per_b)],
        out_specs=pl.BlockSpec((None, tm, d), row),
        scratch_shapes=[pltpu.VMEM((tm + SUBLANES, 2 * d_ff), F32),
                        pltpu.VMEM((tm, d), BF16)],
        compiler_params=pltpu.CompilerParams(
            dimension_semantics=("parallel", "arbitrary"), vmem_limit_bytes=VMEM_LIMIT),
        name="mix_ffn",
    )(y_attn, y_ssm, x, attn_norm_g.reshape(1, d_attn), w_out_a, w_out_s, gate1,
      norm2_g.reshape(1, d), scale2, shift2, w_up, conv_w, conv_b.reshape(1, 2 * d_ff), w_down, gate2)


def _pad_lanes(v, fill=0.0):
    n = v.shape[-1]
    return jnp.pad(v, (0, LANES - n), constant_values=fill).reshape(1, LANES)


def kernel(x, c, w_ada, b_ada, norm1_g, w_in, q_norm_g, k_norm_g, conv_ssm_w, conv_ssm_b, dt_bias, a_log, d_skip, ssm_norm_g, attn_norm_g, w_out, norm2_g, w_up, conv_ffn_w, conv_ffn_b, w_down):
    depth = w_ada.shape[0]
    bsz, seq, d = x.shape
    n_heads_ssm = dt_bias.shape[-1]
    d_ssm = ssm_norm_g.shape[-1]
    d_attn = attn_norm_g.shape[-1]
    d_xbc = conv_ssm_w.shape[-1]
    d_in = w_in.shape[-1]
    assert d_in == 3 * d_attn + d_ssm + d_xbc + n_heads_ssm
    assert seq % MOBA_BLOCK == 0 and seq % SSM_CHUNK == 0
    assert d_ssm == n_heads_ssm * SSM_HEAD_DIM and n_heads_ssm <= LANES
    for l in range(depth):
        mod = _adaln_mod(c, w_ada[l], b_ada[l])
        shift1, scale1, gate1, shift2, scale2, gate2 = (
            m.reshape(bsz, 1, d) for m in jnp.split(mod, 6, axis=-1))
        w_main = w_in[l][:, :d_in - n_heads_ssm].astype(BF16)
        w_dt = jnp.pad(w_in[l][:, d_in - n_heads_ssm:].astype(BF16), ((0, 0), (0, LANES - n_heads_ssm)))
        q, k, vt, z, xbc, dt, kmean = _inproj(x, norm1_g[l], scale1, shift1, w_main, w_dt,
                                              q_norm_g[l], k_norm_g[l],
                                              conv_ssm_w[l], conv_ssm_b[l],
                                              d_attn=d_attn, d_ssm=d_ssm, d_xbc=d_xbc, tm=512)
        y_attn = _moba_attention(q, k, vt, kmean, heads_per_step=4)
        y_ssm = _ssd(xbc, dt, z, _pad_lanes(dt_bias[l]), _pad_lanes(a_log[l]),
                     jnp.repeat(d_skip[l], SSM_HEAD_DIM).reshape(1, d_ssm), ssm_norm_g[l],
                     n_heads=n_heads_ssm)
        w_o = w_out[l].astype(BF16)
        x = _mix_ffn(y_attn, y_ssm, x, attn_norm_g[l], w_o[:d_attn], w_o[d_attn:], gate1,
                     norm2_g[l], scale2, shift2, w_up[l].astype(BF16), conv_ffn_w[l], conv_ffn_b[l],
                     w_down[l].astype(BF16), gate2, tm=512)
    return x
```
